```python
import math
import jax
import jax.numpy as jnp
from jax import lax
import numpy as np

D_MODEL = 2048
BATCH = 4
SEQ = 2048
DEPTH = 2
DEC_BATCH = 128
DEC_SEQ = 8
PAST_LEN = 2048
PAGE_SIZE = 128

PLE_DIM = 256
D_FF = ((8 * D_MODEL // 3 + 255) // 256) * 256
EPS = 1e-6
ROPE_THETA = 10000.0
RET_HEADS = D_MODEL // 256
RET_HEAD_DIM = 128
RET_WIDTH = RET_HEADS * RET_HEAD_DIM
RET_CHUNK = 128
SSM_WIDTH = D_MODEL // 2
SSM_GROUP = 16
SSM_GROUPS = SSM_WIDTH // SSM_GROUP
SSM_STATE = 64
NSA_HEADS = D_MODEL // 128
NSA_KV_HEADS = 4
NSA_HEAD_DIM = 64
NSA_REP = NSA_HEADS // NSA_KV_HEADS
NSA_WIDTH = NSA_HEADS * NSA_HEAD_DIM
KV_WIDTH = NSA_KV_HEADS * NSA_HEAD_DIM
CMP_BLOCK = 32
CMP_STRIDE = 16
CMP_SUB = CMP_BLOCK // CMP_STRIDE
SEL_BLOCK = 64
SEL_TOPK = 16
WINDOW = 512
WIN_QBLOCK = 128
SEL_QBLOCK = 64
SEL_FORCE = 1e4
NEG = -1e30
IN_WIDTHS = (RET_WIDTH,) * 4 + (SSM_WIDTH, NSA_WIDTH) + (KV_WIDTH,) * 6 + (3 * NSA_HEADS, 3 * D_MODEL)
N_IN = sum(IN_WIDTHS)

kernel_name = 'hybrid_retention_s5_nsa_macaron_step'


def split_cols(t, widths):
    offs, acc = [], 0
    for wd in widths[:-1]:
        acc += wd
        offs.append(acc)
    return jnp.split(t, offs, axis=-1)


def rms_norm(x, g):
    xf = x.astype(jnp.float32)
    y = xf * lax.rsqrt(jnp.mean(xf * xf, axis=-1, keepdims=True) + EPS)
    return (y * g.astype(jnp.float32)).astype(x.dtype)


def rope(x, pos):
    half = x.shape[-1] // 2
    inv = ROPE_THETA ** (-jnp.arange(half, dtype=jnp.float32) / half)
    ang = pos.astype(jnp.float32)[:, None] * inv[None, :]
    cos, sin = jnp.cos(ang)[:, None, :], jnp.sin(ang)[:, None, :]
    xf = x.astype(jnp.float32)
    x1, x2 = xf[..., :half], xf[..., half:]
    return jnp.concatenate([x1 * cos - x2 * sin, x1 * sin + x2 * cos], axis=-1).astype(x.dtype)


def masked_softmax(s, mask):
    mf = mask.astype(jnp.float32)
    s = jnp.where(mask, s.astype(jnp.float32), NEG)
    p = jnp.exp(s - jnp.max(s, axis=-1, keepdims=True)) * mf
    return p / jnp.maximum(jnp.sum(p, axis=-1, keepdims=True), 1e-30)


def swiglu_half(x, g, w_gu, w_down):
    a, b = jnp.split(rms_norm(x, g) @ w_gu, 2, axis=-1)
    return x + 0.5 * ((jax.nn.silu(a) * b) @ w_down)


def ret_log_decay():
    return jnp.log1p(-(2.0 ** (-5.0 - jnp.arange(RET_HEADS, dtype=jnp.float32))))


def retention_chunk(S, qkv):
    q, k, v = qkv
    C, dt = q.shape[1], q.dtype
    lg = ret_log_decay()
    n = jnp.arange(C, dtype=jnp.float32)
    diff = n[:, None] - n[None, :]
    dmask = jnp.where(diff >= 0, jnp.exp(lg[:, None, None] * jnp.maximum(diff, 0.0)), 0.0)
    q_dec = jnp.exp(lg[None, :] * (n[:, None] + 1.0))
    k_dec = jnp.exp(lg[None, :] * (C - 1.0 - n[:, None]))
    c_dec = jnp.exp(lg * C)
    scores = jnp.einsum('bihd,bjhd->bhij', q, k) * dmask.astype(dt)
    o = (jnp.einsum('bhij,bjhe->bihe', scores, v)
         + jnp.einsum('bihd,bhde->bihe', q, S) * q_dec[None, :, :, None].astype(dt))
    S_new = (S * c_dec[None, :, None, None].astype(dt)
             + jnp.einsum('bjhd,bjhe->bhde', k * k_dec[None, :, :, None].astype(dt), v))
    return S_new, o


def retention_mixer(q, k, v, gate, S0, pos, gn, chunk):
    B, L = q.shape[:2]
    q = rope(q, pos)
    k = rope(k, pos) * RET_HEAD_DIM ** -0.5
    nc = L // chunk

    def to_chunks(t):
        return t.reshape(B, nc, chunk, *t.shape[2:]).swapaxes(0, 1)

    S, o = lax.scan(retention_chunk, S0, (to_chunks(q), to_chunks(k), to_chunks(v)))
    o = o.swapaxes(0, 1).reshape(B, L, RET_HEADS, RET_HEAD_DIM)
    o = rms_norm(o, gn).reshape(B, L, RET_WIDTH)
    return jax.nn.silu(gate) * o, S


def ssm_mixer(u, x0_re, x0_im, lam_re, lam_im, log_step, b_re, b_im, c_re, c_im, d, w_glu):
    Bsz, L, _ = u.shape
    f32 = jnp.float32
    uf = u.reshape(Bsz, L, SSM_GROUPS, SSM_GROUP).astype(f32)
    step = jnp.exp(log_step.astype(f32))[:, None]
    lr, li = lam_re.astype(f32), lam_im.astype(f32)
    mag = jnp.exp(lr * step)
    ab_re, ab_im = mag * jnp.cos(li * step), mag * jnp.sin(li * step)
    den = lr * lr + li * li
    nr, ni = ab_re - 1.0, ab_im
    f_re, f_im = (nr * lr + ni * li) / den, (ni * lr - nr * li) / den
    b_re, b_im = b_re.astype(f32), b_im.astype(f32)
    bb_re = f_re[..., None] * b_re - f_im[..., None] * b_im
    bb_im = f_re[..., None] * b_im + f_im[..., None] * b_re
    bu_re = jnp.einsum('blgc,gpc->blgp', uf, bb_re)
    bu_im = jnp.einsum('blgc,gpc->blgp', uf, bb_im)
    x0r, x0i = x0_re.astype(f32), x0_im.astype(f32)
    bu_re = bu_re.at[:, 0].add(ab_re * x0r - ab_im * x0i)
    bu_im = bu_im.at[:, 0].add(ab_re * x0i + ab_im * x0r)
    a_re = jnp.broadcast_to(ab_re, bu_re.shape)
    a_im = jnp.broadcast_to(ab_im, bu_im.shape)

    def combine(e1, e2):
        a1r, a1i, b1r, b1i = e1
        a2r, a2i, b2r, b2i = e2
        return (a2r * a1r - a2i * a1i, a2r * a1i + a2i * a1r,
                a2r * b1r - a2i * b1i + b2r, a2r * b1i + a2i * b1r + b2i)

    _, _, xr, xi = lax.associative_scan(combine, (a_re, a_im, bu_re, bu_im), axis=1)
    y = (jnp.einsum('blgp,gcp->blgc', xr, c_re.astype(f32))
         - jnp.einsum('blgp,gcp->blgc', xi, c_im.astype(f32)) + d.astype(f32) * uf)
    z = jax.nn.gelu(y.reshape(Bsz, L, SSM_WIDTH)).astype(u.dtype)
    a, g = jnp.split(z @ w_glu, 2, axis=-1)
    return a * jax.nn.sigmoid(g), xr[:, -1].astype(x0_re.dtype), xi[:, -1].astype(x0_im.dtype)


def nsa_compress(rows, w, pe):
    B, Lt = rows.shape[:2]
    n_sub = Lt // CMP_STRIDE
    n_cmp = n_sub - CMP_SUB + 1
    sub = rows[:, :n_sub * CMP_STRIDE].reshape(B, n_sub, CMP_STRIDE, NSA_KV_HEADS, NSA_HEAD_DIM)
    w_sub = w.reshape(CMP_SUB, CMP_STRIDE, NSA_HEAD_DIM, NSA_HEAD_DIM)
    out = jnp.einsum('ld,lde->e', pe, w)
    for s in range(CMP_SUB):
        out = out + jnp.einsum('bmogd,ode->bmge', sub[:, s:s + n_cmp], w_sub[s])
    return out


def cmp_sel_cover(n_cmp, n_sel):
    i = np.arange(n_cmp)[:, None]
    j = np.arange(n_sel)[None, :]
    start = i * CMP_STRIDE
    end = start + CMP_BLOCK - 1
    m = (start <= (j + 1) * SEL_BLOCK - 1) & (end >= j * SEL_BLOCK)
    return jnp.asarray(m.astype(np.float32))


def sel_attend(qg, kb, vb, idx, qpos):
    B, Q = qg.shape[:2]
    bi = jnp.arange(B)[:, None, None, None]
    gi = jnp.arange(NSA_KV_HEADS)[None, :, None, None]
    kg = kb[bi, gi, idx]
    vg = vb[bi, gi, idx]
    s = jnp.einsum('bqgrd,bgqkld->bgrqkl', qg, kg)
    kpos = idx[..., None] * SEL_BLOCK + jnp.arange(SEL_BLOCK)
    mask = (kpos <= qpos[None, None, :, None, None]).reshape(B, NSA_KV_HEADS, 1, Q, -1)
    p = masked_softmax(s.reshape(*s.shape[:4], -1), mask)
    return jnp.einsum('bgrqn,bgqnd->bqgrd', p.astype(qg.dtype), vg.reshape(B, NSA_KV_HEADS, Q, -1, NSA_HEAD_DIM))


def nsa_global(qg, kc, vc, ks, vs, qpos, cmp_pe, cmp_wk, cmp_wv, map_over_batch):
    B, Lq = qg.shape[:2]
    Lt = kc.shape[1]
    kcmp = nsa_compress(kc, cmp_wk, cmp_pe[0])
    vcmp = nsa_compress(vc, cmp_wv, cmp_pe[1])
    n_cmp = kcmp.shape[1]
    cmp_end = jnp.arange(n_cmp) * CMP_STRIDE + CMP_BLOCK - 1
    cmask = cmp_end[None, :] <= qpos[:, None]
    p = masked_softmax(jnp.einsum('bqgrd,bjgd->bgrqj', qg, kcmp), cmask)
    o_cmp = jnp.einsum('bgrqj,bjgd->bqgrd', p.astype(qg.dtype), vcmp)
    n_sel = -(-Lt // SEL_BLOCK)
    imp = jnp.einsum('bgrqj,js->bgqs', p, cmp_sel_cover(n_cmp, n_sel))
    blk = jnp.arange(n_sel)[None, :]
    valid = blk * SEL_BLOCK <= qpos[:, None]
    cur = qpos[:, None] // SEL_BLOCK
    forced = (blk == 0) | (blk == cur) | (blk == cur - 1)
    score = jnp.where(valid, imp + jnp.where(forced, SEL_FORCE, 0.0), NEG)
    top = min(SEL_TOPK, n_sel)
    _, idx = lax.top_k(score, top)
    pad = n_sel * SEL_BLOCK - Lt

    def blocks(t):
        t = jnp.pad(t, ((0, 0), (0, pad), (0, 0), (0, 0)))
        return t.reshape(B, n_sel, SEL_BLOCK, NSA_KV_HEADS, NSA_HEAD_DIM).transpose(0, 3, 1, 2, 4)

    kb, vb = blocks(ks), blocks(vs)
    if map_over_batch:
        o_sel = lax.map(lambda a: sel_attend(a[0], a[1], a[2], a[3], qpos),
                        (qg[:, None], kb[:, None], vb[:, None], idx[:, None]))[:, 0]
    else:
        nb = Lq // SEL_QBLOCK
        qb = qg.reshape(B, nb, SEL_QBLOCK, NSA_KV_HEADS, NSA_REP, NSA_HEAD_DIM).swapaxes(0, 1)
        ib = idx.reshape(B, NSA_KV_HEADS, nb, SEL_QBLOCK, top).transpose(2, 0, 1, 3, 4)
        pb = qpos.reshape(nb, SEL_QBLOCK)
        o = lax.map(lambda a: sel_attend(a[0], kb, vb, a[1], a[2]), (qb, ib, pb))
        o_sel = o.swapaxes(0, 1).reshape(B, Lq, NSA_KV_HEADS, NSA_REP, NSA_HEAD_DIM)
    return o_cmp, o_sel


def window_prompt(qg, kw, vw):
    B, L = qg.shape[:2]
    nb = L // WIN_QBLOCK
    span = WIN_QBLOCK + WINDOW
    kp = jnp.pad(kw, ((0, 0), (WINDOW, 0), (0, 0), (0, 0)))
    vp = jnp.pad(vw, ((0, 0), (WINDOW, 0), (0, 0), (0, 0)))
    rows = jnp.arange(nb)[:, None] * WIN_QBLOCK + jnp.arange(span)[None, :]
    kpos = rows - WINDOW
    qpos = jnp.arange(L).reshape(nb, WIN_QBLOCK)
    kb, vb = kp[:, rows], vp[:, rows]
    dist = qpos[:, :, None] - kpos[:, None, :]
    mask = (dist >= 0) & (dist < WINDOW) & (kpos[:, None, :] >= 0)
    qb = qg.reshape(B, nb, WIN_QBLOCK, NSA_KV_HEADS, NSA_REP, NSA_HEAD_DIM)
    p = masked_softmax(jnp.einsum('bnqgrd,bnkgd->bngrqk', qb, kb), mask[None, :, None, None])
    o = jnp.einsum('bngrqk,bnkgd->bnqgrd', p.astype(qg.dtype), vb)
    return o.reshape(B, L, NSA_KV_HEADS, NSA_REP, NSA_HEAD_DIM)


def window_sample(qg, k_buf, v_buf, kw, vw, qpos):
    wb = k_buf.shape[1]
    k_all = jnp.concatenate([k_buf, kw], axis=1)
    v_all = jnp.concatenate([v_buf, vw], axis=1)
    kpos = qpos[0] - wb + jnp.arange(k_all.shape[1])
    dist = qpos[:, None] - kpos[None, :]
    mask = (dist >= 0) & (dist < WINDOW)
    p = masked_softmax(jnp.einsum('bqgrd,bkgd->bgrqk', qg, k_all), mask)
    return jnp.einsum('bgrqk,bkgd->bqgrd', p.astype(qg.dtype), v_all)


def trunk_layer(x, ple, pos, w, st):
    B, L, _ = x.shape
    fresh = st is None
    h = swiglu_half(x, w['ffn1_norm'], w['ffn1_w_gu'], w['ffn1_w_down'])
    u = rms_norm(h, w['mix_norm'])
    (r_q, r_k, r_v, r_g, s_u, n_q, n_kc, n_vc, n_ks, n_vs, n_kw, n_vw,
     n_gate, m_gate) = split_cols(u @ w['w_in'], IN_WIDTHS)

    rh = lambda t: t.reshape(B, L, RET_HEADS, RET_HEAD_DIM)
    S0 = jnp.zeros((B, RET_HEADS, RET_HEAD_DIM, RET_HEAD_DIM), x.dtype) if fresh else st['ret']
    o_ret, S_new = retention_mixer(rh(r_q), rh(r_k), rh(r_v), r_g, S0, pos, w['ret_gn'],
                                   RET_CHUNK if fresh else L)

    if fresh:
        x0r = jnp.zeros((B, SSM_GROUPS, SSM_STATE), x.dtype)
        x0i = x0r
    else:
        x0r, x0i = st['ssm_re'], st['ssm_im']
    o_ssm, xr, xi = ssm_mixer(s_u, x0r, x0i, w['ssm_lam_re'], w['ssm_lam_im'], w['ssm_log_step'],
                              w['ssm_b_re'], w['ssm_b_im'], w['ssm_c_re'], w['ssm_c_im'], w['ssm_d'],
                              w['ssm_w_glu'])

    qkn = w['nsa_qk_norm']
    kvh = lambda t: t.reshape(B, L, NSA_KV_HEADS, NSA_HEAD_DIM)
    q = rope(rms_norm(n_q.reshape(B, L, NSA_HEADS, NSA_HEAD_DIM), qkn[0]), pos)
    qg = (q * NSA_HEAD_DIM ** -0.5).reshape(B, L, NSA_KV_HEADS, NSA_REP, NSA_HEAD_DIM)
    kc = rope(rms_norm(kvh(n_kc), qkn[1]), pos)
    ks = rope(rms_norm(kvh(n_ks), qkn[2]), pos)
    kw = rope(rms_norm(kvh(n_kw), qkn[3]), pos)
    vc, vs, vw = kvh(n_vc), kvh(n_vs), kvh(n_vw)
    rows = jnp.stack([kc, vc, ks, vs], axis=2)
    win_rows = jnp.stack([kw, vw], axis=2)
    if fresh:
        kc_a, vc_a, ks_a, vs_a = kc, vc, ks, vs
        o_win = window_prompt(qg, kw, vw)
        win_state = win_rows[:, -min(WINDOW, L):]
    else:
        past = st['kv_past']
        kc_a, vc_a, ks_a, vs_a = [jnp.concatenate([past[:, :, c], t], axis=1)
                                  for c, t in enumerate((kc, vc, ks, vs))]
        buf = st['win']
        o_win = window_sample(qg, buf[:, :, 0], buf[:, :, 1], kw, vw, pos)
        win_state = jnp.concatenate([buf, win_rows], axis=1)[:, -buf.shape[1]:]
    o_cmp, o_sel = nsa_global(qg, kc_a, vc_a, ks_a, vs_a, pos, w['nsa_cmp_pe'],
                              w['nsa_cmp_wk'], w['nsa_cmp_wv'], not fresh)
    bg = jax.nn.sigmoid(n_gate).reshape(B, L, NSA_KV_HEADS, NSA_REP, 3, 1)
    o_nsa = (bg[..., 0, :] * o_cmp + bg[..., 1, :] * o_sel + bg[..., 2, :] * o_win).reshape(B, L, NSA_WIDTH)

    g_ret, g_ssm, g_nsa = jnp.split(jax.nn.sigmoid(m_gate), 3, axis=-1)
    merged = (g_ret * (o_ret @ w['ret_w_o']) + g_ssm * (o_ssm @ w['ssm_w_o'])
              + g_nsa * (o_nsa @ w['nsa_w_o']))
    h = h + merged @ w['w_out']
    h = swiglu_half(h, w['ffn2_norm'], w['ffn2_w_gu'], w['ffn2_w_down'])
    gate = jax.nn.sigmoid(rms_norm(h, w['ple_norm']) @ w['ple_w_gate'])
    h = h + gate * (ple @ w['ple_w_proj'])
    return h, {'kv': rows, 'win': win_state, 'ret': S_new, 'ssm_re': xr, 'ssm_im': xi}


def setup_inputs(seed: int = 0) -> dict:
    key = jax.random.key(seed)
    keys = iter(jax.random.split(key, 64))

    def normal(shape, scale=1.0):
        return jax.random.normal(next(keys), shape, jnp.float32) * scale

    def gain(shape):
        return 1.0 + normal(shape, 0.01)

    n_pages = PAST_LEN // PAGE_SIZE
    n_used = DEC_BATCH * n_pages
    n_pool = n_used + -(-n_used // 4)
    page_table = jax.random.permutation(next(keys), n_pool)[:n_used].reshape(DEC_BATCH, n_pages).astype(jnp.int32)
    win_buf = min(WINDOW, PAST_LEN)
    G, P, HD = SSM_GROUPS, SSM_STATE, NSA_HEAD_DIM
    return {
        'x_prompt': normal((BATCH, SEQ, D_MODEL)),
        'x_sample': normal((DEC_BATCH, DEC_SEQ, D_MODEL)),
        'cache_nsa': normal((DEPTH, n_pool, PAGE_SIZE, 4, NSA_KV_HEADS, HD)),
        'cache_win': normal((DEPTH, DEC_BATCH, win_buf, 2, NSA_KV_HEADS, HD)),
        'state_ret': normal((DEPTH, DEC_BATCH, RET_HEADS, RET_HEAD_DIM, RET_HEAD_DIM)),
        'state_ssm_re': normal((DEPTH, DEC_BATCH, G, P), 0.3),
        'state_ssm_im': normal((DEPTH, DEC_BATCH, G, P), 0.3),
        'page_table': page_table,
        'p_prompt': normal((DEPTH, BATCH, SEQ, PLE_DIM)),
        'p_sample': normal((DEPTH, DEC_BATCH, DEC_SEQ, PLE_DIM)),
        'ffn1_norm': gain((DEPTH, D_MODEL)),
        'ffn1_w_gu': normal((DEPTH, D_MODEL, 2 * D_FF), D_MODEL ** -0.5),
        'ffn1_w_down': normal((DEPTH, D_FF, D_MODEL), D_FF ** -0.5),
        'mix_norm': gain((DEPTH, D_MODEL)),
        'w_in': normal((DEPTH, D_MODEL, N_IN), D_MODEL ** -0.5),
        'ret_gn': gain((DEPTH, RET_HEADS, RET_HEAD_DIM)),
        'ret_w_o': normal((DEPTH, RET_WIDTH, D_MODEL), RET_WIDTH ** -0.5),
        'ssm_lam_re': -0.5 + normal((DEPTH, G, P), 0.01),
        'ssm_lam_im': math.pi * jnp.arange(P, dtype=jnp.float32) + normal((DEPTH, G, P), 0.01),
        'ssm_log_step': jax.random.uniform(next(keys), (DEPTH, G), jnp.float32, math.log(1e-3), math.log(1e-1)),
        'ssm_b_re': normal((DEPTH, G, P, SSM_GROUP), (2 * SSM_GROUP) ** -0.5),
        'ssm_b_im': normal((DEPTH, G, P, SSM_GROUP), (2 * SSM_GROUP) ** -0.5),
        'ssm_c_re': normal((DEPTH, G, SSM_GROUP, P), P ** -0.5),
        'ssm_c_im': normal((DEPTH, G, SSM_GROUP, P), P ** -0.5),
        'ssm_d': normal((DEPTH, G, SSM_GROUP)),
        'ssm_w_glu': normal((DEPTH, SSM_WIDTH, 2 * SSM_WIDTH), SSM_WIDTH ** -0.5),
        'ssm_w_o': normal((DEPTH, SSM_WIDTH, D_MODEL), SSM_WIDTH ** -0.5),
        'nsa_qk_norm': gain((DEPTH, 4, HD)),
        'nsa_cmp_pe': normal((DEPTH, 2, CMP_BLOCK, HD), 0.02),
        'nsa_cmp_wk': normal((DEPTH, CMP_BLOCK, HD, HD), (CMP_BLOCK * HD) ** -0.5),
        'nsa_cmp_wv': normal((DEPTH, CMP_BLOCK, HD, HD), (CMP_BLOCK * HD) ** -0.5),
        'nsa_w_o': normal((DEPTH, NSA_WIDTH, D_MODEL), NSA_WIDTH ** -0.5),
        'w_out': normal((DEPTH, D_MODEL, D_MODEL), D_MODEL ** -0.5),
        'ffn2_norm': gain((DEPTH, D_MODEL)),
        'ffn2_w_gu': normal((DEPTH, D_MODEL, 2 * D_FF), D_MODEL ** -0.5),
        'ffn2_w_down': normal((DEPTH, D_FF, D_MODEL), D_FF ** -0.5),
        'ple_norm': gain((DEPTH, D_MODEL)),
        'ple_w_gate': normal((DEPTH, D_MODEL, D_MODEL), D_MODEL ** -0.5),
        'ple_w_proj': normal((DEPTH, PLE_DIM, D_MODEL), PLE_DIM ** -0.5),
    }


def reference(x_prompt, x_sample, cache_nsa, cache_win, state_ret, state_ssm_re, state_ssm_im,
              page_table, p_prompt, p_sample,
              ffn1_norm, ffn1_w_gu, ffn1_w_down, mix_norm, w_in, ret_gn, ret_w_o,
              ssm_lam_re, ssm_lam_im, ssm_log_step, ssm_b_re, ssm_b_im, ssm_c_re, ssm_c_im, ssm_d,
              ssm_w_glu, ssm_w_o, nsa_qk_norm, nsa_cmp_pe, nsa_cmp_wk, nsa_cmp_wv, nsa_w_o, w_out,
              ffn2_norm, ffn2_w_gu, ffn2_w_down, ple_norm, ple_w_gate, ple_w_proj):
    names = ('ffn1_norm', 'ffn1_w_gu', 'ffn1_w_down', 'mix_norm', 'w_in', 'ret_gn', 'ret_w_o',
             'ssm_lam_re', 'ssm_lam_im', 'ssm_log_step', 'ssm_b_re', 'ssm_b_im', 'ssm_c_re', 'ssm_c_im',
             'ssm_d', 'ssm_w_glu', 'ssm_w_o', 'nsa_qk_norm', 'nsa_cmp_pe', 'nsa_cmp_wk', 'nsa_cmp_wv',
             'nsa_w_o', 'w_out', 'ffn2_norm', 'ffn2_w_gu', 'ffn2_w_down', 'ple_norm', 'ple_w_gate',
             'ple_w_proj')
    params = (ffn1_norm, ffn1_w_gu, ffn1_w_down, mix_norm, w_in, ret_gn, ret_w_o,
              ssm_lam_re, ssm_lam_im, ssm_log_step, ssm_b_re, ssm_b_im, ssm_c_re, ssm_c_im,
              ssm_d, ssm_w_glu, ssm_w_o, nsa_qk_norm, nsa_cmp_pe, nsa_cmp_wk, nsa_cmp_wv,
              nsa_w_o, w_out, ffn2_norm, ffn2_w_gu, ffn2_w_down, ple_norm, ple_w_gate, ple_w_proj)
    n_dec, n_pages = page_table.shape
    past_len = n_pages * PAGE_SIZE
    pos_p = jnp.arange(x_prompt.shape[1], dtype=jnp.int32)
    pos_s = past_len + jnp.arange(x_sample.shape[1], dtype=jnp.int32)
    hp, hs = x_prompt, x_sample
    new_p, new_s = [], []
    for i in range(DEPTH):
        w = {n: a[i] for n, a in zip(names, params)}
        hp, sp = trunk_layer(hp, p_prompt[i], pos_p, w, None)
        past = cache_nsa[i][page_table].reshape(n_dec, past_len, 4, NSA_KV_HEADS, NSA_HEAD_DIM)
        carried = {'kv_past': past, 'win': cache_win[i], 'ret': state_ret[i],
                   'ssm_re': state_ssm_re[i], 'ssm_im': state_ssm_im[i]}
        hs, ss = trunk_layer(hs, p_sample[i], pos_s, w, carried)
        new_p.append(sp)
        new_s.append(ss)

    def stk(states, name):
        return jnp.stack([s[name] for s in states])

    return (hp, hs,
            stk(new_p, 'kv'), stk(new_p, 'win'), stk(new_p, 'ret'), stk(new_p, 'ssm_re'), stk(new_p, 'ssm_im'),
            stk(new_s, 'kv'), stk(new_s, 'win'), stk(new_s, 'ret'), stk(new_s, 'ssm_re'), stk(new_s, 'ssm_im'))
```

```python
import functools
import math

import numpy as np
import jax
import jax.numpy as jnp
from jax import lax
from jax.experimental import pallas as pl
from jax.experimental.pallas import tpu as pltpu

F32 = jnp.float32
BF16 = jnp.bfloat16

D_MODEL = 2048
PAGE_SIZE = 128
PLE_DIM = 256
D_FF = 5632
EPS = 1e-6
ROPE_THETA = 10000.0
RET_HEADS = 8
RET_HEAD_DIM = 128
RET_WIDTH = 1024
RET_CHUNK = 128
SSM_WIDTH = 1024
SSM_GROUP = 16
SSM_GROUPS = 64
SSM_STATE = 64
NSA_HEADS = 16
NSA_KV_HEADS = 4
NSA_HEAD_DIM = 64
NSA_REP = 4
NSA_WIDTH = 1024
KV_WIDTH = 256
CMP_BLOCK = 32
CMP_STRIDE = 16
SEL_BLOCK = 64
SEL_TOPK = 16
WINDOW = 512
SEL_FORCE = 1e4
NEG = -1e30

LANES = 128
VMEM_LIMIT = 56 * 1024 * 1024
N_MAIN = 7680
N_MGATE = 3 * D_MODEL
N_NGATE = 3 * NSA_HEADS
N_PROJ = 14336
COL_NGATE = N_MAIN + N_MGATE
TN = 512


def _cp(*sem):
    return pltpu.CompilerParams(dimension_semantics=sem, vmem_limit_bytes=VMEM_LIMIT)


def _dot(a, b):
    return jnp.dot(a.astype(BF16), b.astype(BF16), preferred_element_type=F32)


def _dot_nt(a, b):
    return lax.dot_general(a.astype(BF16), b.astype(BF16), (((1,), (1,)), ((), ())),
                           preferred_element_type=F32)


def _dot_exact_rhs(a, b_bf16):
    hi = a.astype(BF16)
    lo = (a - hi.astype(F32)).astype(BF16)
    return (jnp.dot(hi, b_bf16, preferred_element_type=F32)
            + jnp.dot(lo, b_bf16, preferred_element_type=F32))


def _sigmoid(x):
    return 1.0 / (1.0 + jnp.exp(-x))


def _rms_rows(x, g):
    return x * lax.rsqrt(jnp.mean(x * x, axis=-1, keepdims=True) + EPS) * g


def _iota(shape, dim):
    return lax.broadcasted_iota(jnp.int32, shape, dim)


def _masked_softmax(s, mask):
    mf = mask.astype(F32)
    s = jnp.where(mask, s, NEG)
    p = jnp.exp(s - jnp.max(s, axis=-1, keepdims=True)) * mf
    return p / jnp.maximum(jnp.sum(p, axis=-1, keepdims=True), 1e-30)


def _ffn_body(x_ref, g_ref, wa_ref, wb_ref, wd_ref, o_ref, xn_ref, acc_ref):
    j = pl.program_id(1)

    @pl.when(j == 0)
    def _():
        xn_ref[...] = _rms_rows(x_ref[...], g_ref[...]).astype(BF16)
        acc_ref[...] = jnp.zeros_like(acc_ref)

    xn = xn_ref[...]
    a = jnp.dot(xn, wa_ref[...], preferred_element_type=F32)
    b = jnp.dot(xn, wb_ref[...], preferred_element_type=F32)
    h = (a * _sigmoid(a)) * b
    acc_ref[...] += jnp.dot(h.astype(BF16), wd_ref[...], preferred_element_type=F32)

    @pl.when(j == pl.num_programs(1) - 1)
    def _():
        o_ref[...] = x_ref[...] + 0.5 * acc_ref[...]


def _ffn(x, g, w_gu, w_down, tm):
    m, d = x.shape
    f = w_down.shape[0]
    nj = f // TN
    return pl.pallas_call(
        _ffn_body,
        grid=(m // tm, nj),
        in_specs=[pl.BlockSpec((tm, d), lambda i, j: (i, 0)),
                  pl.BlockSpec((1, d), lambda i, j: (0, 0)),
                  pl.BlockSpec((d, TN), lambda i, j: (0, j)),
                  pl.BlockSpec((d, TN), lambda i, j: (0, j + nj)),
                  pl.BlockSpec((TN, d), lambda i, j: (j, 0))],
        out_specs=pl.BlockSpec((tm, d), lambda i, j: (i, 0)),
        out_shape=jax.ShapeDtypeStruct((m, d), F32),
        scratch_shapes=[pltpu.VMEM((tm, d), BF16), pltpu.VMEM((tm, d), F32)],
        compiler_params=_cp("parallel", "arbitrary"),
        name="ffn",
    )(x, g, w_gu, w_gu, w_down)


def _proj_body(x_ref, g_ref, w_ref, o_ref, xn_ref):
    @pl.when(pl.program_id(1) == 0)
    def _():
        xn_ref[...] = _rms_rows(x_ref[...], g_ref[...]).astype(BF16)

    o_ref[...] = jnp.dot(xn_ref[...], w_ref[...], preferred_element_type=F32)


def _proj(x, g, w, tm):
    m, d = x.shape
    n = w.shape[1]
    return pl.pallas_call(
        _proj_body,
        grid=(m // tm, n // TN),
        in_specs=[pl.BlockSpec((tm, d), lambda i, j: (i, 0)),
                  pl.BlockSpec((1, d), lambda i, j: (0, 0)),
                  pl.BlockSpec((d, TN), lambda i, j: (0, j))],
        out_specs=pl.BlockSpec((tm, TN), lambda i, j: (i, j)),
        out_shape=jax.ShapeDtypeStruct((m, n), F32),
        scratch_shapes=[pltpu.VMEM((tm, d), BF16)],
        compiler_params=_cp("parallel", "arbitrary"),
        name="in_proj",
    )(x, g, w)


def _ple_body(h_ref, ht_ref, g_ref, p_ref, wg_ref, wp_ref, o_ref, xn_ref):
    @pl.when(pl.program_id(1) == 0)
    def _():
        xn_ref[...] = _rms_rows(h_ref[...], g_ref[...]).astype(BF16)

    gate = _sigmoid(jnp.dot(xn_ref[...], wg_ref[...], preferred_element_type=F32))
    o_ref[...] = ht_ref[...] + gate * _dot(p_ref[...], wp_ref[...])


def _ple(h, g, p, w_gate, w_proj, tm):
    m, d = h.shape
    pd = p.shape[1]
    return pl.pallas_call(
        _ple_body,
        grid=(m // tm, d // TN),
        in_specs=[pl.BlockSpec((tm, d), lambda i, j: (i, 0)),
                  pl.BlockSpec((tm, TN), lambda i, j: (i, j)),
                  pl.BlockSpec((1, d), lambda i, j: (0, 0)),
                  pl.BlockSpec((tm, pd), lambda i, j: (i, 0)),
                  pl.BlockSpec((d, TN), lambda i, j: (0, j)),
                  pl.BlockSpec((pd, TN), lambda i, j: (0, j))],
        out_specs=pl.BlockSpec((tm, TN), lambda i, j: (i, j)),
        out_shape=jax.ShapeDtypeStruct((m, d), F32),
        scratch_shapes=[pltpu.VMEM((tm, d), BF16)],
        compiler_params=_cp("parallel", "arbitrary"),
        name="ple",
    )(h, h, g, p, w_gate, w_proj)


def _merge_body(h_ref, oret_ref, ossm_ref, ocmp_ref, osel_ref, owin_ref, ng_ref,
                gr_ref, gs_ref, gn_ref, wr_ref, ws_ref, wn_ref, wo_ref, o_ref,
                ret_s, ssm_s, nsa_s, acc_ref):
    j = pl.program_id(1)

    @pl.when(j == 0)
    def _():
        ret_s[...] = oret_ref[...].astype(BF16)
        ssm_s[...] = ossm_ref[...].astype(BF16)
        bg = _sigmoid(ng_ref[...])
        row = _iota((LANES, NSA_WIDTH), 0)
        head = _iota((LANES, NSA_WIDTH), 1) >> 6
        nsa = jnp.zeros(nsa_s.shape, F32)
        for c, ref in enumerate((ocmp_ref, osel_ref, owin_ref)):
            expand = jnp.where(row == 3 * head + c, 1.0, 0.0).astype(BF16)
            nsa = nsa + _dot_exact_rhs(bg, expand) * ref[...]
        nsa_s[...] = nsa.astype(BF16)
        acc_ref[...] = jnp.zeros_like(acc_ref)

    merged = (_sigmoid(gr_ref[...]) * jnp.dot(ret_s[...], wr_ref[...], preferred_element_type=F32)
              + _sigmoid(gs_ref[...]) * jnp.dot(ssm_s[...], ws_ref[...], preferred_element_type=F32)
              + _sigmoid(gn_ref[...]) * jnp.dot(nsa_s[...], wn_ref[...], preferred_element_type=F32))
    acc_ref[...] += jnp.dot(merged.astype(BF16), wo_ref[...], preferred_element_type=F32)

    @pl.when(j == pl.num_programs(1) - 1)
    def _():
        o_ref[...] = h_ref[...] + acc_ref[...]


def _merge(h, proj, o_ret, o_ssm, o_cmp, o_sel, o_win, w_ret, w_ssm, w_nsa, w_out, tm):
    m, d = h.shape
    nj = d // TN
    gbase = N_MAIN // TN
    wide = lambda i, j: (i, 0)
    return pl.pallas_call(
        _merge_body,
        grid=(m // tm, nj),
        in_specs=[pl.BlockSpec((tm, d), wide),
                  pl.BlockSpec((tm, RET_WIDTH), wide),
                  pl.BlockSpec((tm, SSM_WIDTH), wide),
                  pl.BlockSpec((tm, NSA_WIDTH), wide),
                  pl.BlockSpec((tm, NSA_WIDTH), wide),
                  pl.BlockSpec((tm, NSA_WIDTH), wide),
                  pl.BlockSpec((tm, LANES), lambda i, j: (i, COL_NGATE // LANES)),
                  pl.BlockSpec((tm, TN), lambda i, j: (i, gbase + j)),
                  pl.BlockSpec((tm, TN), lambda i, j: (i, gbase + nj + j)),
                  pl.BlockSpec((tm, TN), lambda i, j: (i, gbase + 2 * nj + j)),
                  pl.BlockSpec((RET_WIDTH, TN), lambda i, j: (0, j)),
                  pl.BlockSpec((SSM_WIDTH, TN), lambda i, j: (0, j)),
                  pl.BlockSpec((NSA_WIDTH, TN), lambda i, j: (0, j)),
                  pl.BlockSpec((TN, d), lambda i, j: (j, 0))],
        out_specs=pl.BlockSpec((tm, d), wide),
        out_shape=jax.ShapeDtypeStruct((m, d), F32),
        scratch_shapes=[pltpu.VMEM((tm, RET_WIDTH), BF16), pltpu.VMEM((tm, SSM_WIDTH), BF16),
                        pltpu.VMEM((tm, NSA_WIDTH), BF16), pltpu.VMEM((tm, d), F32)],
        compiler_params=_cp("parallel", "arbitrary"),
        name="merge",
    )(h, o_ret, o_ssm, o_cmp, o_sel, o_win, proj, proj, proj, proj, w_ret, w_ssm, w_nsa, w_out)


def _rope_tables(pos, head_dim):
    half = head_dim // 2
    inv = ROPE_THETA ** (-jnp.arange(half, dtype=F32) / half)
    ang = pos.astype(F32)[:, None] * inv[None, :]
    cos, sin = jnp.cos(ang), jnp.sin(ang)
    reps = LANES // head_dim
    return (jnp.tile(jnp.concatenate([cos, cos], axis=-1), (1, reps)),
            jnp.tile(jnp.concatenate([-sin, sin], axis=-1), (1, reps)))


def _rope64(x, cos, sin, first_half):
    partner = jnp.where(first_half, pltpu.roll(x, 96, 1), pltpu.roll(x, 32, 1))
    return x * cos + partner * sin


def _rope128(x, cos, sin):
    return x * cos + pltpu.roll(x, 64, 1) * sin


def _head_rms(x, gain, seg):
    ms = _dot_exact_rhs(x * x, seg) * (1.0 / NSA_HEAD_DIM)
    return x * lax.rsqrt(ms + EPS) * gain


def _pack_group(k256, v256, g, lo):
    kc = k256[:, (g // 2) * LANES:(g // 2 + 1) * LANES]
    vc = v256[:, (g // 2) * LANES:(g // 2 + 1) * LANES]
    if g % 2 == 0:
        return jnp.where(lo, kc, pltpu.roll(vc, 64, 1))
    return jnp.where(lo, pltpu.roll(kc, 64, 1), vc)


def _prep_body(q_ref, kv_ref, cos_ref, sin_ref, gain_ref, qo_ref, rows_ref, win_ref, *pack_refs):
    tm = q_ref.shape[0]
    cos, sin = cos_ref[...], sin_ref[...]
    lane = _iota((tm, LANES), 1)
    lo = lane < 64
    first_half = (lane & 63) < 32
    seg = jnp.where((_iota((LANES, LANES), 0) >> 6) == (_iota((LANES, LANES), 1) >> 6), 1.0, 0.0).astype(BF16)

    def normed(x, c):
        return _rope64(_head_rms(x, gain_ref[c:c + 1, :], seg), cos, sin, first_half)

    for v in range(NSA_WIDTH // LANES):
        sl = slice(v * LANES, (v + 1) * LANES)
        qo_ref[:, sl] = normed(q_ref[:, sl], 0) * NSA_HEAD_DIM ** -0.5
    kv = []
    for c in range(6):
        cols = []
        for v in range(2):
            x = kv_ref[:, c * KV_WIDTH + v * LANES:c * KV_WIDTH + (v + 1) * LANES]
            if c % 2 == 0:
                x = normed(x, 1 + c // 2)
            cols.append(x)
        kv.append(jnp.concatenate(cols, axis=1))
    for c in range(4):
        rows_ref[:, c * KV_WIDTH:(c + 1) * KV_WIDTH] = kv[c]
    for c in range(2):
        win_ref[:, c * KV_WIDTH:(c + 1) * KV_WIDTH] = kv[4 + c]
    if pack_refs:
        sel_ref, wpk_ref = pack_refs
        for g in range(NSA_KV_HEADS):
            sel_ref[0, g] = _pack_group(kv[2], kv[3], g, lo)
            wpk_ref[0, g] = _pack_group(kv[4], kv[5], g, lo)


def _nsa_prep(proj, cos, sin, gain, tm, seq_len=None):
    m = proj.shape[0]
    nper = cos.shape[0] // tm
    out_shape = [jax.ShapeDtypeStruct((m, NSA_WIDTH), F32),
                 jax.ShapeDtypeStruct((m, 4 * KV_WIDTH), F32),
                 jax.ShapeDtypeStruct((m, 2 * KV_WIDTH), F32)]
    out_specs = [pl.BlockSpec((tm, NSA_WIDTH), lambda i: (i, 0)),
                 pl.BlockSpec((tm, 4 * KV_WIDTH), lambda i: (i, 0)),
                 pl.BlockSpec((tm, 2 * KV_WIDTH), lambda i: (i, 0))]
    if seq_len is not None:
        nt = seq_len // tm
        for _ in range(2):
            out_shape.append(jax.ShapeDtypeStruct((m // seq_len, NSA_KV_HEADS, seq_len, LANES), F32))
            out_specs.append(pl.BlockSpec((1, NSA_KV_HEADS, tm, LANES), lambda i: (i // nt, 0, i % nt, 0)))
    return pl.pallas_call(
        _prep_body,
        grid=(m // tm,),
        in_specs=[pl.BlockSpec((tm, NSA_WIDTH), lambda i: (i, 5)),
                  pl.BlockSpec((tm, 6 * KV_WIDTH), lambda i: (i, 4)),
                  pl.BlockSpec((tm, LANES), lambda i: (i % nper, 0)),
                  pl.BlockSpec((tm, LANES), lambda i: (i % nper, 0)),
                  pl.BlockSpec((4, LANES), lambda i: (0, 0))],
        out_specs=out_specs,
        out_shape=out_shape,
        compiler_params=_cp("parallel"),
        name="nsa_prep",
    )(proj, proj, cos, sin, gain)


def _ret_body(c_real, q_ref, k_ref, v_ref, gate_ref, cos_ref, sin_ref, s0_ref, gn_ref, lg_ref,
              o_ref, sout_ref, s_scr):
    r = q_ref.shape[0]
    p = max(r, RET_CHUNK)

    @pl.when(pl.program_id(1) == 0)
    def _():
        s_scr[...] = s0_ref[0]

    cos, sin = cos_ref[...], sin_ref[...]
    n_row = _iota((p, LANES), 0).astype(F32)
    diff = (_iota((p, p), 0) - _iota((p, p), 1)).astype(F32)

    def pad(x):
        if r == p:
            return x
        return jnp.concatenate([x, jnp.zeros((p - r, x.shape[1]), F32)], axis=0)

    for h in range(RET_HEADS):
        sl = slice(h * RET_HEAD_DIM, (h + 1) * RET_HEAD_DIM)
        lg = lg_ref[h:h + 1, :]
        q = pad(_rope128(q_ref[:, sl], cos, sin))
        k = pad(_rope128(k_ref[:, sl], cos, sin) * RET_HEAD_DIM ** -0.5)
        v = pad(v_ref[:, sl])
        dmask = jnp.where(diff >= 0, jnp.exp(lg * jnp.maximum(diff, 0.0)), 0.0)
        scores = _dot_nt(q, k) * dmask
        s_old = s_scr[h]
        o = _dot(scores, v) + _dot(q, s_old) * jnp.exp(lg * (n_row + 1.0))
        kd = k * jnp.exp(lg * (c_real - 1.0 - n_row))
        s_scr[h] = s_old * jnp.exp(lg * float(c_real)) + _dot(kd.T, v)
        o = _rms_rows(o[:r], gn_ref[h:h + 1, :])
        gate = gate_ref[:, sl]
        o_ref[:, sl] = (gate * _sigmoid(gate)) * o

    sout_ref[0] = s_scr[...]


def _retention(proj, cos, sin, s0, gn, lg, nseq, rows):
    m = proj.shape[0]
    nchunk = m // (nseq * rows)
    blk = lambda col: pl.BlockSpec((rows, RET_WIDTH), lambda s, c: (s * nchunk + c, col))
    full = lambda s, c: (0, 0)
    return pl.pallas_call(
        functools.partial(_ret_body, rows),
        grid=(nseq, nchunk),
        in_specs=[blk(0), blk(1), blk(2), blk(3),
                  pl.BlockSpec((rows, LANES), lambda s, c: (c, 0)),
                  pl.BlockSpec((rows, LANES), lambda s, c: (c, 0)),
                  pl.BlockSpec((1, RET_HEADS, RET_HEAD_DIM, RET_HEAD_DIM), lambda s, c: (s, 0, 0, 0)),
                  pl.BlockSpec((RET_HEADS, LANES), full),
                  pl.BlockSpec((RET_HEADS, LANES), full)],
        out_specs=[pl.BlockSpec((rows, RET_WIDTH), lambda s, c: (s * nchunk + c, 0)),
                   pl.BlockSpec((1, RET_HEADS, RET_HEAD_DIM, RET_HEAD_DIM), lambda s, c: (s, 0, 0, 0))],
        out_shape=[jax.ShapeDtypeStruct((m, RET_WIDTH), F32),
                   jax.ShapeDtypeStruct(s0.shape, F32)],
        scratch_shapes=[pltpu.VMEM((RET_HEADS, RET_HEAD_DIM, RET_HEAD_DIM), F32)],
        compiler_params=_cp("parallel", "arbitrary"),
        name="retention",
    )(proj, proj, proj, proj, cos, sin, s0, gn, lg)


SSM_TILE_GROUPS = LANES // SSM_GROUP
SSM_TILES = SSM_GROUPS // SSM_TILE_GROUPS
SSM_TILE_STATE = SSM_TILE_GROUPS * SSM_STATE
SSM_NSTATE = SSM_GROUPS * SSM_STATE
SSM_SUB = SSM_TILE_STATE // LANES


def _gelu_tanh(x):
    return 0.5 * x * (1.0 + jnp.tanh(math.sqrt(2.0 / math.pi) * (x + 0.044715 * (x * x * x))))


def _ssm_body(u_ref, x0r_ref, x0i_ref, ar_ref, ai_ref, wre_ref, wim_ref, cre_ref, cim_ref, d_ref,
              wglu_ref, o_ref, xr_ref, xi_ref, bur, bui, y_scr, st_r, st_i):
    nb, t_len, _ = u_ref.shape
    rows = nb * t_len

    @pl.when(pl.program_id(1) == 0)
    def _():
        st_r[...] = x0r_ref[...]
        st_i[...] = x0i_ref[...]

    for k in range(SSM_TILES):
        usl = slice(k * LANES, (k + 1) * LANES)
        ssl = slice(k * SSM_TILE_STATE, (k + 1) * SSM_TILE_STATE)
        uk = u_ref[:, :, usl].reshape(rows, LANES)
        bu_r = _dot(uk, wre_ref[k])
        bu_i = _dot(uk, wim_ref[k])
        subs = [slice(k * SSM_TILE_STATE + j * LANES, k * SSM_TILE_STATE + (j + 1) * LANES)
                for j in range(SSM_SUB)]
        for j in range(SSM_SUB):
            bur[j] = bu_r[:, j * LANES:(j + 1) * LANES]
            bui[j] = bu_i[:, j * LANES:(j + 1) * LANES]
        ar = [jnp.broadcast_to(ar_ref[:, s], (nb, LANES)) for s in subs]
        ai = [jnp.broadcast_to(ai_ref[:, s], (nb, LANES)) for s in subs]

        def step(t, carry):
            xr, xi = carry
            idx = pl.ds(t, nb, stride=t_len)
            nr, ni = [], []
            for j in range(SSM_SUB):
                nr.append(ar[j] * xr[j] - ai[j] * xi[j] + bur[j, idx, :])
                ni.append(ar[j] * xi[j] + ai[j] * xr[j] + bui[j, idx, :])
                bur[j, idx, :] = nr[j]
                bui[j, idx, :] = ni[j]
            return tuple(nr), tuple(ni)

        xr, xi = lax.fori_loop(0, t_len, step, (tuple(st_r[:, s] for s in subs),
                                                tuple(st_i[:, s] for s in subs)))
        for j, s in enumerate(subs):
            st_r[:, s] = xr[j]
            st_i[:, s] = xi[j]
        xs_r = jnp.concatenate([bur[j] for j in range(SSM_SUB)], axis=1)
        xs_i = jnp.concatenate([bui[j] for j in range(SSM_SUB)], axis=1)
        y_scr[:, usl] = _dot(xs_r, cre_ref[k]) - _dot(xs_i, cim_ref[k]) + d_ref[:, usl] * uk

    z = _gelu_tanh(y_scr[...])
    gl = _dot(z, wglu_ref[...])
    out = gl[:, :SSM_WIDTH] * _sigmoid(gl[:, SSM_WIDTH:])
    o_ref[...] = out.reshape(nb, t_len, SSM_WIDTH)
    xr_ref[...] = st_r[...]
    xi_ref[...] = st_i[...]


def _ssm(proj3, x0r, x0i, consts, nb, t_len):
    nseq, seq_len, _ = proj3.shape
    rows = nb * t_len
    ar, ai, wre, wim, cre, cim, dvec, wglu = consts
    full2 = lambda s, c: (0, 0)
    full3 = lambda s, c: (0, 0, 0)
    state = pl.BlockSpec((nb, SSM_NSTATE), lambda s, c: (s, 0))
    return pl.pallas_call(
        _ssm_body,
        grid=(nseq // nb, seq_len // t_len),
        in_specs=[pl.BlockSpec((nb, t_len, SSM_WIDTH), lambda s, c: (s, c, 4)),
                  state, state,
                  pl.BlockSpec((1, SSM_NSTATE), full2), pl.BlockSpec((1, SSM_NSTATE), full2),
                  pl.BlockSpec(wre.shape, full3), pl.BlockSpec(wim.shape, full3),
                  pl.BlockSpec(cre.shape, full3), pl.BlockSpec(cim.shape, full3),
                  pl.BlockSpec((1, SSM_WIDTH), full2),
                  pl.BlockSpec(wglu.shape, full2)],
        out_specs=[pl.BlockSpec((nb, t_len, SSM_WIDTH), lambda s, c: (s, c, 0)), state, state],
        out_shape=[jax.ShapeDtypeStruct((nseq, seq_len, SSM_WIDTH), F32),
                   jax.ShapeDtypeStruct((nseq, SSM_NSTATE), F32),
                   jax.ShapeDtypeStruct((nseq, SSM_NSTATE), F32)],
        scratch_shapes=[pltpu.VMEM((SSM_SUB, rows, LANES), F32), pltpu.VMEM((SSM_SUB, rows, LANES), F32),
                        pltpu.VMEM((rows, SSM_WIDTH), F32),
                        pltpu.VMEM((nb, SSM_NSTATE), F32), pltpu.VMEM((nb, SSM_NSTATE), F32)],
        compiler_params=_cp("parallel", "arbitrary"),
        name="ssm",
    )(proj3, x0r, x0i, ar, ai, wre, wim, cre, cim, dvec, wglu)


def _ssm_consts(lam_re, lam_im, log_step, b_re, b_im, c_re, c_im, d, w_glu):
    step = jnp.exp(log_step)[:, None]
    mag = jnp.exp(lam_re * step)
    ab_re, ab_im = mag * jnp.cos(lam_im * step), mag * jnp.sin(lam_im * step)
    den = lam_re * lam_re + lam_im * lam_im
    nr, ni = ab_re - 1.0, ab_im
    f_re, f_im = (nr * lam_re + ni * lam_im) / den, (ni * lam_re - nr * lam_im) / den
    bb_re = f_re[..., None] * b_re - f_im[..., None] * b_im
    bb_im = f_re[..., None] * b_im + f_im[..., None] * b_re
    eye = jnp.eye(SSM_TILE_GROUPS, dtype=F32)

    def in_map(bb):
        t = bb.reshape(SSM_TILES, SSM_TILE_GROUPS, SSM_STATE, SSM_GROUP)
        w = jnp.einsum('kgpc,gh->kgchp', t, eye)
        return w.reshape(SSM_TILES, LANES, SSM_TILE_STATE).astype(BF16)

    def out_map(cc):
        t = cc.reshape(SSM_TILES, SSM_TILE_GROUPS, SSM_GROUP, SSM_STATE)
        w = jnp.einsum('kgcp,gh->kgphc', t, eye)
        return w.reshape(SSM_TILES, SSM_TILE_STATE, LANES).astype(BF16)

    return (ab_re.reshape(1, SSM_NSTATE), ab_im.reshape(1, SSM_NSTATE),
            in_map(bb_re), in_map(bb_im), out_map(c_re), out_map(c_im),
            d.reshape(1, SSM_WIDTH), w_glu.astype(BF16))


def _compress_rows(read_rows, wk_ref, wv_ref, pek, pev, n_sub, lo):
    out = []
    for c, (w_ref, pe) in enumerate(((wk_ref, pek), (wv_ref, pev))):
        a = jnp.zeros((n_sub, KV_WIDTH), F32)
        b = jnp.zeros((n_sub, KV_WIDTH), F32)
        for o in range(CMP_STRIDE):
            x = read_rows(o, c).astype(BF16)
            a = a + jnp.dot(x, w_ref[o], preferred_element_type=F32)
            b = b + jnp.dot(x, w_ref[CMP_STRIDE + o], preferred_element_type=F32)
        out.append(a + pltpu.roll(b, n_sub - 1, 0) + pe)
    return [_pack_group(out[0], out[1], g, lo) for g in range(NSA_KV_HEADS)]


def _pe_term(pe_ref, w_ref):
    acc = jnp.zeros((8, KV_WIDTH), F32)
    for l in range(CMP_BLOCK):
        row = jnp.broadcast_to(pe_ref[l:l + 1, :], (8, KV_WIDTH)).astype(BF16)
        acc = acc + jnp.dot(row, w_ref[l], preferred_element_type=F32)
    return acc[0:1, :]


def _cmp_prompt_body(rows_ref, wk_ref, wv_ref, pek_ref, pev_ref, o_ref):
    n_sub = o_ref.shape[2]
    lo = _iota((n_sub, LANES), 1) < 64
    pek = _pe_term(pek_ref, wk_ref)
    pev = _pe_term(pev_ref, wv_ref)

    def read_rows(o, c):
        base = o * 4 * KV_WIDTH + c * KV_WIDTH
        return rows_ref[0, :, base:base + KV_WIDTH]

    packs = _compress_rows(read_rows, wk_ref, wv_ref, pek, pev, n_sub, lo)
    for g in range(NSA_KV_HEADS):
        o_ref[0, g] = packs[g]


def _compress_prompt(rows3, wk, wv, pek, pev):
    b, seq_len, width = rows3.shape
    n_sub = seq_len // CMP_STRIDE
    rows3 = rows3.reshape(b, n_sub, CMP_STRIDE * width)
    full3 = lambda i: (0, 0, 0)
    full2 = lambda i: (0, 0)
    return pl.pallas_call(
        _cmp_prompt_body,
        grid=(b,),
        in_specs=[pl.BlockSpec((1, n_sub, CMP_STRIDE * width), lambda i: (i, 0, 0)),
                  pl.BlockSpec(wk.shape, full3), pl.BlockSpec(wv.shape, full3),
                  pl.BlockSpec(pek.shape, full2), pl.BlockSpec(pev.shape, full2)],
        out_specs=pl.BlockSpec((1, NSA_KV_HEADS, n_sub, LANES), lambda i: (i, 0, 0, 0)),
        out_shape=jax.ShapeDtypeStruct((b, NSA_KV_HEADS, n_sub, LANES), F32),
        compiler_params=_cp("parallel"),
        name="nsa_compress",
    )(rows3, wk, wv, pek, pev)


def _cover_matrix(n_cmp, n_sel):
    i = np.arange(LANES)[:, None]
    j = np.arange(LANES)[None, :]
    start = i * CMP_STRIDE
    end = start + CMP_BLOCK - 1
    m = (start <= (j + 1) * SEL_BLOCK - 1) & (end >= j * SEL_BLOCK) & (i < n_cmp) & (j < n_sel)
    return jnp.asarray(m.astype(np.float32), dtype=BF16)


def _select_blocks(imp, qpos, n_sel):
    nq = imp.shape[0]
    blk = _iota((nq, LANES), 1)
    valid = blk * SEL_BLOCK <= qpos
    cur = qpos >> 6
    forced = (blk == 0) | (blk == cur) | (blk == cur - 1)
    score = jnp.where(valid, imp + jnp.where(forced, SEL_FORCE, 0.0), NEG)
    rank = jnp.zeros((nq, LANES), F32)
    for s in range(n_sel):
        col = jnp.broadcast_to(score[:, s:s + 1], (nq, LANES))
        ahead = (col > score) | ((col == score) & (blk > s))
        rank = rank + jnp.where(ahead, 1.0, 0.0)
    top = min(SEL_TOPK, n_sel)
    return jnp.where((rank < top) & (blk < n_sel), 1.0, 0.0)


def _split_heads(q256, lo):
    v0, v1 = q256[:, :LANES], q256[:, LANES:]
    return [jnp.where(lo, v0, 0.0), jnp.where(lo, pltpu.roll(v0, 64, 1), 0.0),
            jnp.where(lo, v1, 0.0), jnp.where(lo, pltpu.roll(v1, 64, 1), 0.0)]


def _join_heads(o4, lo):
    return jnp.concatenate([jnp.where(lo, pltpu.roll(o4[0], 64, 1), o4[1]),
                            jnp.where(lo, pltpu.roll(o4[2], 64, 1), o4[3])], axis=1)


TQ = 128


def _flash_tiles(qall, kv_ref, lo_tile, hi_tile, mask_fn, m_scr, l_scr, acc_scr):
    m_scr[...] = jnp.full(m_scr.shape, NEG, F32)
    l_scr[...] = jnp.zeros(l_scr.shape, F32)
    acc_scr[...] = jnp.zeros(acc_scr.shape, F32)

    def body(kt, carry):
        kv = kv_ref[0, 0, pl.ds(pl.multiple_of(kt * TQ, TQ), TQ), :]
        mask = mask_fn(kt)
        s = jnp.where(mask, _dot_nt(qall, kv), NEG)
        m_old = m_scr[...]
        m_new = jnp.maximum(m_old, jnp.max(s, axis=-1, keepdims=True))
        alpha = jnp.exp(m_old - m_new)
        p = jnp.exp(s - m_new) * mask.astype(F32)
        l_scr[...] = alpha * l_scr[...] + jnp.sum(p, axis=-1, keepdims=True)
        acc_scr[...] = alpha * acc_scr[...] + _dot(p, kv)
        m_scr[...] = m_new
        return carry

    lax.fori_loop(lo_tile, hi_tile, body, 0)
    return acc_scr[...] / jnp.maximum(l_scr[...], 1e-30)


def _nsa_prompt_body(n_cmp, n_sel, q_ref, cmp_ref, sel_ref, win_ref, cover_ref,
                     ocmp_ref, osel_ref, owin_ref, m_scr, l_scr, acc_scr):
    i = pl.program_id(2)
    rows = NSA_REP * TQ
    lo = _iota((TQ, LANES), 1) < 64
    qall = jnp.concatenate(_split_heads(q_ref[...], lo), axis=0)
    qpos1 = i * TQ + _iota((TQ, 1), 0)
    qpos = jnp.concatenate([qpos1] * NSA_REP, axis=0)
    lane = _iota((rows, LANES), 1)

    ckv = cmp_ref[0, 0]
    cmask = (lane * CMP_STRIDE + (CMP_BLOCK - 1) <= qpos) & (lane < n_cmp)
    p = _masked_softmax(_dot_nt(qall, ckv), cmask)
    ocmp = _dot(p, ckv)
    imp4 = _dot(p, cover_ref[...])
    imp = imp4[0:TQ] + imp4[TQ:2 * TQ] + imp4[2 * TQ:3 * TQ] + imp4[3 * TQ:4 * TQ]
    sel = _select_blocks(imp, qpos1, n_sel).astype(BF16)
    ocmp_ref[...] = _join_heads([ocmp[r * TQ:(r + 1) * TQ] for r in range(NSA_REP)], lo)

    blk_row = _iota((LANES, LANES), 0)
    key_blk = _iota((LANES, LANES), 1) >> 6

    def sel_mask(kt):
        expand = jnp.where(blk_row == 2 * kt + key_blk, 1.0, 0.0).astype(BF16)
        chosen = jnp.dot(sel, expand, preferred_element_type=F32)
        chosen = jnp.concatenate([chosen] * NSA_REP, axis=0)
        return (chosen > 0.5) & (kt * TQ + lane <= qpos)

    osel = _flash_tiles(qall, sel_ref, 0, i + 1, sel_mask, m_scr, l_scr, acc_scr)
    osel_ref[...] = _join_heads([osel[r * TQ:(r + 1) * TQ] for r in range(NSA_REP)], lo)

    def win_mask(kt):
        dist = qpos - (kt * TQ + lane)
        return (dist >= 0) & (dist < WINDOW)

    owin = _flash_tiles(qall, win_ref, jnp.maximum(i - WINDOW // TQ, 0), i + 1, win_mask,
                        m_scr, l_scr, acc_scr)
    owin_ref[...] = _join_heads([owin[r * TQ:(r + 1) * TQ] for r in range(NSA_REP)], lo)


def _nsa_prompt(q, cmp_pack, sel_pack, win_pack, seq_len):
    m = q.shape[0]
    b = m // seq_len
    nq = seq_len // TQ
    n_sub = cmp_pack.shape[2]
    n_cmp = n_sub - 1
    n_sel = -(-seq_len // SEL_BLOCK)
    cover = _cover_matrix(n_cmp, n_sel)
    rows = NSA_REP * TQ
    oblk = pl.BlockSpec((TQ, KV_WIDTH), lambda bi, g, i: (bi * nq + i, g))
    oshape = jax.ShapeDtypeStruct((m, NSA_WIDTH), F32)
    return pl.pallas_call(
        functools.partial(_nsa_prompt_body, n_cmp, n_sel),
        grid=(b, NSA_KV_HEADS, nq),
        in_specs=[pl.BlockSpec((TQ, KV_WIDTH), lambda bi, g, i: (bi * nq + i, g)),
                  pl.BlockSpec((1, 1, n_sub, LANES), lambda bi, g, i: (bi, g, 0, 0)),
                  pl.BlockSpec((1, 1, seq_len, LANES), lambda bi, g, i: (bi, g, 0, 0)),
                  pl.BlockSpec((1, 1, seq_len, LANES), lambda bi, g, i: (bi, g, 0, 0)),
                  pl.BlockSpec((LANES, LANES), lambda bi, g, i: (0, 0))],
        out_specs=[oblk, oblk, oblk],
        out_shape=[oshape, oshape, oshape],
        scratch_shapes=[pltpu.VMEM((rows, 1), F32), pltpu.VMEM((rows, 1), F32),
                        pltpu.VMEM((rows, LANES), F32)],
        compiler_params=_cp("parallel", "parallel", "arbitrary"),
        name="nsa_prompt",
    )(q, cmp_pack, sel_pack, win_pack, cover)


def _nsa_sample_body(n_pages, past_len, pt_ref, page_ref, q_ref, kvn_ref, wn_ref, cw_ref,
                     wk_ref, wv_ref, pek_ref, pev_ref, cover_ref, kpos_ref, expand_ref,
                     ocmp_ref, osel_ref, owin_ref, wout_ref, kc_scr, vc_scr, sel_scr, win_scr):
    pg = pl.program_id(1)
    r = q_ref.shape[0]
    wb = cw_ref.shape[1]
    n_sub = past_len // CMP_STRIDE
    n_cmp = (past_len + r) // CMP_STRIDE - 1
    n_sel = -(-(past_len + r) // SEL_BLOCK)
    sel_rows = sel_scr.shape[1]
    win_rows = win_scr.shape[1]
    sub_page = PAGE_SIZE // CMP_STRIDE
    lo_page = _iota((sub_page, LANES), 1) < 64

    sub0 = pl.multiple_of(pg * sub_page, sub_page)
    for o in range(CMP_STRIDE):
        base = o * 4 * KV_WIDTH
        kc_scr[o, pl.ds(sub0, sub_page), :] = page_ref[0, :, base:base + KV_WIDTH]
        vc_scr[o, pl.ds(sub0, sub_page), :] = page_ref[0, :, base + KV_WIDTH:base + 2 * KV_WIDTH]
        ks = page_ref[0, :, base + 2 * KV_WIDTH:base + 3 * KV_WIDTH]
        vs = page_ref[0, :, base + 3 * KV_WIDTH:base + 4 * KV_WIDTH]
        for g in range(NSA_KV_HEADS):
            sel_scr[g, pl.ds(o * n_sub + sub0, sub_page), :] = _pack_group(ks, vs, g, lo_page)

    @pl.when(pg == n_pages - 1)
    def _():
        lo_r = _iota((r, LANES), 1) < 64
        kvn = kvn_ref[...]
        wn = wn_ref[...]
        cw = cw_ref[0]
        tail = sel_rows - past_len - r
        for g in range(NSA_KV_HEADS):
            sel_scr[g, past_len:past_len + r, :] = _pack_group(
                kvn[:, 2 * KV_WIDTH:3 * KV_WIDTH], kvn[:, 3 * KV_WIDTH:], g, lo_r)
            sel_scr[g, past_len + r:, :] = jnp.zeros((tail, LANES), F32)
            win_scr[g, 0:wb, :] = _pack_group(cw[:, :KV_WIDTH], cw[:, KV_WIDTH:], g,
                                              _iota((wb, LANES), 1) < 64)
            win_scr[g, wb:wb + r, :] = _pack_group(wn[:, :KV_WIDTH], wn[:, KV_WIDTH:], g, lo_r)
            win_scr[g, wb + r:, :] = jnp.zeros((win_rows - wb - r, LANES), F32)

        wout_ref[0, 0:wb - r, :] = cw[r:, :]
        wout_ref[0, wb - r:, :] = wn

        pek = _pe_term(pek_ref, wk_ref)
        pev = _pe_term(pev_ref, wv_ref)

        def read_rows(o, c):
            return kc_scr[o] if c == 0 else vc_scr[o]

        cmp_packs = _compress_rows(read_rows, wk_ref, wv_ref, pek, pev, n_sub,
                                   _iota((n_sub, LANES), 1) < 64)

        rows = NSA_REP * r
        q = q_ref[...]
        qpos1 = past_len + _iota((r, 1), 0)
        qpos = jnp.concatenate([qpos1] * NSA_REP, axis=0)
        lane = _iota((rows, LANES), 1)
        key_sel = jnp.broadcast_to(kpos_ref[...], (rows, sel_rows))
        key_win = _iota((rows, win_rows), 1)
        expand = expand_ref[...]
        outs = ([], [], [])
        for g in range(NSA_KV_HEADS):
            qall = jnp.concatenate(_split_heads(q[:, g * KV_WIDTH:(g + 1) * KV_WIDTH], lo_r), axis=0)
            ckv = cmp_packs[g]
            cmask = (lane * CMP_STRIDE + (CMP_BLOCK - 1) <= qpos) & (lane < n_cmp)
            p = _masked_softmax(_dot_nt(qall, ckv), cmask)
            outs[0].append(_dot(p, ckv))
            imp4 = _dot(p, cover_ref[...])
            imp = imp4[0:r]
            for rr in range(1, NSA_REP):
                imp = imp + imp4[rr * r:(rr + 1) * r]
            sel = _select_blocks(imp, qpos1, n_sel).astype(BF16)
            skv = sel_scr[g]
            chosen = jnp.dot(sel, expand, preferred_element_type=F32)
            chosen = jnp.concatenate([chosen] * NSA_REP, axis=0)
            smask = (chosen > 0.5) & (key_sel <= qpos)
            p = _masked_softmax(_dot_nt(qall, skv), smask)
            outs[1].append(_dot(p, skv))
            wkv = win_scr[g]
            dist = qpos - (past_len - wb + key_win)
            wmask = (dist >= 0) & (dist < WINDOW) & (key_win < wb + r)
            p = _masked_softmax(_dot_nt(qall, wkv), wmask)
            outs[2].append(_dot(p, wkv))
        for ref, per_group in zip((ocmp_ref, osel_ref, owin_ref), outs):
            ref[...] = jnp.concatenate(
                [_join_heads([o[rr * r:(rr + 1) * r] for rr in range(NSA_REP)], lo_r) for o in per_group],
                axis=1)


def _nsa_sample(page_table, cache_pages, page_off, q, kv_new, win_new, cache_win, wk, wv, pek, pev):
    nb, n_pages = page_table.shape
    past_len = n_pages * PAGE_SIZE
    r = q.shape[0] // nb
    wb = cache_win.shape[1]
    n_cmp = (past_len + r) // CMP_STRIDE - 1
    n_sel = -(-(past_len + r) // SEL_BLOCK)
    cover = _cover_matrix(n_cmp, n_sel)
    sel_rows = past_len + LANES
    win_rows = wb + LANES
    n_sub = past_len // CMP_STRIDE
    idx = np.arange(sel_rows)
    kpos_np = np.where(idx < past_len, (idx % n_sub) * CMP_STRIDE + idx // n_sub,
                       np.where(idx < past_len + r, idx, np.iinfo(np.int32).max // 2))
    kpos = jnp.asarray(kpos_np[None, :], dtype=jnp.int32)
    expand = jnp.asarray((np.arange(LANES)[:, None] == (kpos_np[None, :] // SEL_BLOCK)).astype(np.float32),
                         dtype=BF16)
    sub_page = PAGE_SIZE // CMP_STRIDE
    cache_pages = cache_pages.reshape(-1, sub_page, CMP_STRIDE * 4 * KV_WIDTH)
    per_b = lambda b, p, pt: (b, 0)
    full3 = lambda b, p, pt: (0, 0, 0)
    full2 = lambda b, p, pt: (0, 0)
    oshape = jax.ShapeDtypeStruct((nb * r, NSA_WIDTH), F32)
    grid_spec = pltpu.PrefetchScalarGridSpec(
        num_scalar_prefetch=1,
        grid=(nb, n_pages),
        in_specs=[pl.BlockSpec((1, sub_page, CMP_STRIDE * 4 * KV_WIDTH),
                               lambda b, p, pt: (pt[b, p] + page_off, 0, 0)),
                  pl.BlockSpec((r, NSA_WIDTH), per_b),
                  pl.BlockSpec((r, 4 * KV_WIDTH), per_b),
                  pl.BlockSpec((r, 2 * KV_WIDTH), per_b),
                  pl.BlockSpec((1, wb, 2 * KV_WIDTH), lambda b, p, pt: (b, 0, 0)),
                  pl.BlockSpec(wk.shape, full3), pl.BlockSpec(wv.shape, full3),
                  pl.BlockSpec(pek.shape, full2), pl.BlockSpec(pev.shape, full2),
                  pl.BlockSpec((LANES, LANES), full2),
                  pl.BlockSpec((1, sel_rows), full2), pl.BlockSpec((LANES, sel_rows), full2)],
        out_specs=[pl.BlockSpec((r, NSA_WIDTH), per_b)] * 3
        + [pl.BlockSpec((1, wb, 2 * KV_WIDTH), lambda b, p, pt: (b, 0, 0))],
        scratch_shapes=[pltpu.VMEM((CMP_STRIDE, n_sub, KV_WIDTH), F32),
                        pltpu.VMEM((CMP_STRIDE, n_sub, KV_WIDTH), F32),
                        pltpu.VMEM((NSA_KV_HEADS, sel_rows, LANES), F32),
                        pltpu.VMEM((NSA_KV_HEADS, win_rows, LANES), F32)])
    return pl.pallas_call(
        functools.partial(_nsa_sample_body, n_pages, past_len),
        grid_spec=grid_spec,
        out_shape=[oshape, oshape, oshape, jax.ShapeDtypeStruct(cache_win.shape, F32)],
        compiler_params=_cp("parallel", "arbitrary"),
        name="nsa_sample",
    )(page_table, cache_pages, q, kv_new, win_new, cache_win, wk, wv, pek, pev, cover, kpos, expand)


def _layer_params(w):
    w_in = w['w_in']
    pad = N_PROJ - (N_MAIN + N_MGATE + N_NGATE)
    w_in = jnp.concatenate([w_in[:, :N_MAIN], w_in[:, N_MAIN + N_NGATE:],
                            w_in[:, N_MAIN:N_MAIN + N_NGATE],
                            jnp.zeros((D_MODEL, pad), w_in.dtype)], axis=1).astype(BF16)
    eye = jnp.eye(NSA_KV_HEADS, dtype=F32)

    def cmp_w(wc):
        return jnp.einsum('lde,gh->lgdhe', wc, eye).reshape(CMP_BLOCK, KV_WIDTH, KV_WIDTH).astype(BF16)

    lg = jnp.log1p(-(2.0 ** (-5.0 - jnp.arange(RET_HEADS, dtype=F32))))
    row = lambda v: v.reshape(1, -1)
    return dict(
        ffn1_norm=row(w['ffn1_norm']), ffn1_w_gu=w['ffn1_w_gu'].astype(BF16),
        ffn1_w_down=w['ffn1_w_down'].astype(BF16),
        mix_norm=row(w['mix_norm']), w_in=w_in,
        ret_gn=w['ret_gn'], ret_lg=jnp.broadcast_to(lg[:, None], (RET_HEADS, LANES)),
        ret_w_o=w['ret_w_o'].astype(BF16),
        ssm=_ssm_consts(w['ssm_lam_re'], w['ssm_lam_im'], w['ssm_log_step'], w['ssm_b_re'], w['ssm_b_im'],
                        w['ssm_c_re'], w['ssm_c_im'], w['ssm_d'], w['ssm_w_glu']),
        ssm_w_o=w['ssm_w_o'].astype(BF16),
        qk_gain=jnp.tile(w['nsa_qk_norm'], (1, 2)),
        cmp_wk=cmp_w(w['nsa_cmp_wk']), cmp_wv=cmp_w(w['nsa_cmp_wv']),
        cmp_pek=jnp.tile(w['nsa_cmp_pe'][0], (1, NSA_KV_HEADS)),
        cmp_pev=jnp.tile(w['nsa_cmp_pe'][1], (1, NSA_KV_HEADS)),
        nsa_w_o=w['nsa_w_o'].astype(BF16), w_out=w['w_out'].astype(BF16),
        ffn2_norm=row(w['ffn2_norm']), ffn2_w_gu=w['ffn2_w_gu'].astype(BF16),
        ffn2_w_down=w['ffn2_w_down'].astype(BF16),
        ple_norm=row(w['ple_norm']), ple_w_gate=w['ple_w_gate'].astype(BF16),
        ple_w_proj=w['ple_w_proj'].astype(BF16))


def _mix_tail(x, h, proj, o_ret, o_ssm, o_cmp, o_sel, o_win, ple, lp, tm):
    h = _merge(h, proj, o_ret, o_ssm, o_cmp, o_sel, o_win,
               lp['ret_w_o'], lp['ssm_w_o'], lp['nsa_w_o'], lp['w_out'], min(tm, 256))
    h = _ffn(h, lp['ffn2_norm'], lp['ffn2_w_gu'], lp['ffn2_w_down'], tm)
    return _ple(h, lp['ple_norm'], ple, lp['ple_w_gate'], lp['ple_w_proj'], tm)


def _prompt_layer(x, ple, lp, tabs, batch, seq_len):
    m = x.shape[0]
    tm = min(512, m)
    h = _ffn(x, lp['ffn1_norm'], lp['ffn1_w_gu'], lp['ffn1_w_down'], tm)
    proj = _proj(h, lp['mix_norm'], lp['w_in'], tm)
    s0 = jnp.zeros((batch, RET_HEADS, RET_HEAD_DIM, RET_HEAD_DIM), F32)
    o_ret, s_new = _retention(proj, tabs['cos_r'], tabs['sin_r'], s0, lp['ret_gn'], lp['ret_lg'],
                              batch, RET_CHUNK)
    zeros = jnp.zeros((batch, SSM_NSTATE), F32)
    o_ssm, xr, xi = _ssm(proj.reshape(batch, seq_len, N_PROJ), zeros, zeros, lp['ssm'], batch, 128)
    q, rows, win_rows, sel_pack, win_pack = _nsa_prep(proj, tabs['cos_n'], tabs['sin_n'], lp['qk_gain'],
                                                      tm, seq_len)
    cmp_pack = _compress_prompt(rows.reshape(batch, seq_len, 4 * KV_WIDTH), lp['cmp_wk'], lp['cmp_wv'],
                                lp['cmp_pek'], lp['cmp_pev'])
    o_cmp, o_sel, o_win = _nsa_prompt(q, cmp_pack, sel_pack, win_pack, seq_len)
    y = _mix_tail(x, h, proj, o_ret, o_ssm.reshape(m, SSM_WIDTH), o_cmp, o_sel, o_win, ple, lp, tm)
    wrows = win_rows.reshape(batch, seq_len, 2, NSA_KV_HEADS, NSA_HEAD_DIM)[:, -min(WINDOW, seq_len):]
    state = dict(kv=rows.reshape(batch, seq_len, 4, NSA_KV_HEADS, NSA_HEAD_DIM), win=wrows, ret=s_new,
                 ssm_re=xr.reshape(batch, SSM_GROUPS, SSM_STATE), ssm_im=xi.reshape(batch, SSM_GROUPS, SSM_STATE))
    return y, state


def _sample_layer(x, ple, lp, tabs, carried, nb, r):
    m = x.shape[0]
    tm = min(512, m)
    h = _ffn(x, lp['ffn1_norm'], lp['ffn1_w_gu'], lp['ffn1_w_down'], tm)
    proj = _proj(h, lp['mix_norm'], lp['w_in'], tm)
    o_ret, s_new = _retention(proj, tabs['cos_r'], tabs['sin_r'], carried['ret'], lp['ret_gn'], lp['ret_lg'],
                              nb, r)
    o_ssm, xr, xi = _ssm(proj.reshape(nb, r, N_PROJ), carried['ssm_re'].reshape(nb, SSM_NSTATE),
                         carried['ssm_im'].reshape(nb, SSM_NSTATE), lp['ssm'], 16, r)
    q, rows, win_rows = _nsa_prep(proj, tabs['cos_n'], tabs['sin_n'], lp['qk_gain'], tm)
    wb = carried['win'].shape[1]
    o_cmp, o_sel, o_win, win_state = _nsa_sample(
        carried['page_table'], carried['cache'], carried['page_off'], q, rows, win_rows,
        carried['win'].reshape(nb, wb, 2 * KV_WIDTH), lp['cmp_wk'], lp['cmp_wv'], lp['cmp_pek'], lp['cmp_pev'])
    y = _mix_tail(x, h, proj, o_ret, o_ssm.reshape(m, SSM_WIDTH), o_cmp, o_sel, o_win, ple, lp, tm)
    state = dict(kv=rows.reshape(nb, r, 4, NSA_KV_HEADS, NSA_HEAD_DIM),
                 win=win_state.reshape(nb, wb, 2, NSA_KV_HEADS, NSA_HEAD_DIM), ret=s_new,
                 ssm_re=xr.reshape(nb, SSM_GROUPS, SSM_STATE), ssm_im=xi.reshape(nb, SSM_GROUPS, SSM_STATE))
    return y, state


def kernel(x_prompt, x_sample, cache_nsa, cache_win, state_ret, state_ssm_re, state_ssm_im, page_table, p_prompt, p_sample, ffn1_norm, ffn1_w_gu, ffn1_w_down, mix_norm, w_in, ret_gn, ret_w_o, ssm_lam_re, ssm_lam_im, ssm_log_step, ssm_b_re, ssm_b_im, ssm_c_re, ssm_c_im, ssm_d, ssm_w_glu, ssm_w_o, nsa_qk_norm, nsa_cmp_pe, nsa_cmp_wk, nsa_cmp_wv, nsa_w_o, w_out, ffn2_norm, ffn2_w_gu, ffn2_w_down, ple_norm, ple_w_gate, ple_w_proj):
    names = ('ffn1_norm', 'ffn1_w_gu', 'ffn1_w_down', 'mix_norm', 'w_in', 'ret_gn', 'ret_w_o',
             'ssm_lam_re', 'ssm_lam_im', 'ssm_log_step', 'ssm_b_re', 'ssm_b_im', 'ssm_c_re', 'ssm_c_im',
             'ssm_d', 'ssm_w_glu', 'ssm_w_o', 'nsa_qk_norm', 'nsa_cmp_pe', 'nsa_cmp_wk', 'nsa_cmp_wv',
             'nsa_w_o', 'w_out', 'ffn2_norm', 'ffn2_w_gu', 'ffn2_w_down', 'ple_norm', 'ple_w_gate',
             'ple_w_proj')
    params = (ffn1_norm, ffn1_w_gu, ffn1_w_down, mix_norm, w_in, ret_gn, ret_w_o,
              ssm_lam_re, ssm_lam_im, ssm_log_step, ssm_b_re, ssm_b_im, ssm_c_re, ssm_c_im,
              ssm_d, ssm_w_glu, ssm_w_o, nsa_qk_norm, nsa_cmp_pe, nsa_cmp_wk, nsa_cmp_wv,
              nsa_w_o, w_out, ffn2_norm, ffn2_w_gu, ffn2_w_down, ple_norm, ple_w_gate, ple_w_proj)
    depth = w_in.shape[0]
    batch, seq_len, d = x_prompt.shape
    nb, r, _ = x_sample.shape
    n_pages = page_table.shape[1]
    past_len = n_pages * PAGE_SIZE
    pos_p = jnp.arange(seq_len, dtype=jnp.int32)
    pos_s = past_len + jnp.arange(r, dtype=jnp.int32)
    tile_s = lambda t: jnp.tile(t, (min(512, nb * r) // r, 1))
    tabs_p, tabs_s = {}, {}
    tabs_p['cos_n'], tabs_p['sin_n'] = _rope_tables(pos_p, NSA_HEAD_DIM)
    tabs_p['cos_r'], tabs_p['sin_r'] = _rope_tables(pos_p, RET_HEAD_DIM)
    cn, sn = _rope_tables(pos_s, NSA_HEAD_DIM)
    tabs_s['cos_n'], tabs_s['sin_n'] = tile_s(cn), tile_s(sn)
    tabs_s['cos_r'], tabs_s['sin_r'] = _rope_tables(pos_s, RET_HEAD_DIM)

    n_pool = cache_nsa.shape[1]
    cache_pages = cache_nsa.reshape(depth * n_pool, PAGE_SIZE, 4 * KV_WIDTH)
    hp = x_prompt.reshape(batch * seq_len, d)
    hs = x_sample.reshape(nb * r, d)
    new_p, new_s = [], []
    for i in range(depth):
        lp = _layer_params({n: a[i] for n, a in zip(names, params)})
        hp, sp = _prompt_layer(hp, p_prompt[i].reshape(batch * seq_len, PLE_DIM), lp, tabs_p, batch, seq_len)
        carried = dict(page_table=page_table, cache=cache_pages, page_off=i * n_pool, win=cache_win[i],
                       ret=state_ret[i],
                       ssm_re=state_ssm_re[i], ssm_im=state_ssm_im[i])
        hs, ss = _sample_layer(hs, p_sample[i].reshape(nb * r, PLE_DIM), lp, tabs_s, carried, nb, r)
        new_p.append(sp)
        new_s.append(ss)

    def stk(states, name):
        return jnp.stack([s[name] for s in states])

    return (hp.reshape(batch, seq_len, d), hs.reshape(nb, r, d),
            stk(new_p, 'kv'), stk(new_p, 'win'), stk(new_p, 'ret'), stk(new_p, 'ssm_re'), stk(new_p, 'ssm_im'),
            stk(new_s, 'kv'), stk(new_s, 'win'), stk(new_s, 'ret'), stk(new_s, 'ssm_re'), stk(new_s, 'ssm_im'))
```
